```python
import math, functools
import jax, jax.numpy as jnp
from jax import lax
import numpy as np

D_MODEL = 1024
BATCH = 4
SEQ = 4096
DEPTH = 4
DEC_BATCH = 32
DEC_SEQ = 8
PAST_LEN = 8192
PAGE_SIZE = 128

HEAD_DIM = 64
MIX_W = D_MODEL
FA_HEADS = MIX_W // 2 // HEAD_DIM
FA_W = FA_HEADS * HEAD_DIM
SC_GROUPS = MIX_W // 4 // HEAD_DIM
SC_W = SC_GROUPS * HEAD_DIM
ML_W = MIX_W - FA_W - SC_W
ML_HEADS = ML_W // HEAD_DIM
SC_WIDTH = 3
FFN_CONV_WIDTH = 3
D_FF = ((8 * D_MODEL // 3 + 127) // 128) * 128
MEM_TOKENS = 256
X_HEADS = 4
X_HEAD_DIM = D_MODEL // X_HEADS
X_W = X_HEADS * X_HEAD_DIM
Q_BLOCK = 128
ML_CHUNK = 128
DN_ALPHA = (2.0 * DEPTH) ** 0.25
DN_BETA = (8.0 * DEPTH) ** -0.25
LN_EPS = 1e-5
FOX_F_BIAS = 4.0
ML_F_BIAS = 4.0
IN_SIZES = (FA_W, FA_W, FA_W, FA_HEADS, SC_W, SC_W, SC_W, ML_W, ML_W, ML_W, ML_W, ML_HEADS, ML_HEADS)
N_IN = sum(IN_SIZES)

kernel_name = 'hymba_fox_conv_mlstm_deepnorm_step'


def layer_norm(x, g, b):
    xf = x.astype(jnp.float32)
    mu = jnp.mean(xf, axis=-1, keepdims=True)
    var = jnp.mean(jnp.square(xf - mu), axis=-1, keepdims=True)
    return ((xf - mu) * lax.rsqrt(var + LN_EPS) * g + b).astype(x.dtype)


def head_norm(h, g):
    B, L, H, Dh = h.shape
    hf = h.astype(jnp.float32)
    mu = jnp.mean(hf, axis=-1, keepdims=True)
    var = jnp.mean(jnp.square(hf - mu), axis=-1, keepdims=True)
    return ((hf - mu) * lax.rsqrt(var + LN_EPS)).reshape(B, L, H * Dh) * g


def split_in(z):
    pts = [int(v) for v in np.cumsum(IN_SIZES)[:-1]]
    return jnp.split(z, pts, axis=-1)


def causal_dwconv(u, w, past):
    width = w.shape[0]
    L = u.shape[1]
    xp = jnp.concatenate([past.astype(u.dtype), u], axis=1)
    out = sum(w[j] * xp[:, j:j + L] for j in range(width))
    return out, xp[:, L:]


def fox_prompt(q, k, v, logf):
    B, S, H, Dh = q.shape
    F = jnp.cumsum(logf, axis=1)
    Ft = jnp.transpose(F, (0, 2, 1))
    nblk = S // Q_BLOCK
    qb = jnp.moveaxis(q.reshape(B, nblk, Q_BLOCK, H, Dh), 1, 0)
    Fb = jnp.moveaxis(Ft.reshape(B, H, nblk, Q_BLOCK), 2, 0)
    pos_k = jnp.arange(S)
    scale = Dh ** -0.5

    def block(args):
        qi, Fi, j = args
        pos_q = j * Q_BLOCK + jnp.arange(Q_BLOCK)
        s = jnp.einsum('bqhd,bkhd->bhqk', qi, k, preferred_element_type=jnp.float32) * scale
        s = s + Fi[..., :, None] - Ft[:, :, None, :]
        s = jnp.where(pos_k[None, :] <= pos_q[:, None], s, -jnp.inf)
        p = jax.nn.softmax(s, axis=-1)
        return jnp.einsum('bhqk,bkhd->bqhd', p.astype(v.dtype), v)

    out = lax.map(block, (qb, Fb, jnp.arange(nblk)))
    return jnp.moveaxis(out, 0, 1).reshape(B, S, H, Dh)


def fox_sample(q, k, v, logf, k_pool, v_pool, lf_pool, page_table):
    DB, L, H, Dh = q.shape
    kp = k_pool[page_table].reshape(DB, -1, H, Dh)
    vp = v_pool[page_table].reshape(DB, -1, H, Dh)
    lfp = lf_pool[page_table].reshape(DB, -1, H).astype(jnp.float32)
    P = kp.shape[1]
    Fp = jnp.cumsum(lfp, axis=1)
    Fn = Fp[:, -1:] + jnp.cumsum(logf, axis=1)
    Fpt = jnp.transpose(Fp, (0, 2, 1))
    Fnt = jnp.transpose(Fn, (0, 2, 1))
    scale = Dh ** -0.5
    s_past = jnp.einsum('bqhd,bkhd->bhqk', q, kp, preferred_element_type=jnp.float32) * scale
    s_past = s_past + Fnt[..., :, None] - Fpt[:, :, None, :]
    s_new = jnp.einsum('bqhd,bkhd->bhqk', q, k, preferred_element_type=jnp.float32) * scale
    s_new = s_new + Fnt[..., :, None] - Fnt[:, :, None, :]
    s_new = jnp.where(jnp.tril(jnp.ones((L, L), bool)), s_new, -jnp.inf)
    p = jax.nn.softmax(jnp.concatenate([s_past, s_new], axis=-1), axis=-1)
    return (jnp.einsum('bhqk,bkhd->bqhd', p[..., :P].astype(vp.dtype), vp)
            + jnp.einsum('bhqk,bkhd->bqhd', p[..., P:].astype(v.dtype), v))


def mlstm(q, k, v, ig, lf, C0, n0, m0):
    B, L, H, Dh = q.shape
    c = L if L <= ML_CHUNK else math.gcd(L, ML_CHUNK)
    nc = L // c

    def to_chunks(a):
        a = jnp.moveaxis(a.reshape((B, nc, c) + a.shape[2:]), 1, 0)
        return jnp.moveaxis(a, 3, 2)

    f32 = jnp.float32
    qc = to_chunks(q.astype(f32))
    kc = to_chunks(k.astype(f32) * Dh ** -0.5)
    vc = to_chunks(v.astype(f32))
    igc, lfc = to_chunks(ig), to_chunks(lf)
    tri = jnp.tril(jnp.ones((c, c), bool))

    def step(carry, xs):
        C, n, m = carry
        qi, ki, vi, ii, fi = xs
        b = jnp.cumsum(fi, axis=-1)
        a = b + m[..., None]
        D = jnp.where(tri, b[..., :, None] - b[..., None, :] + ii[..., None, :], -jnp.inf)
        mt = jnp.maximum(a, jnp.max(D, axis=-1))
        wi = jnp.exp(a - mt)
        S = jnp.einsum('bhtd,bhsd->bhts', qi, ki) * jnp.exp(D - mt[..., None])
        num = wi[..., None] * jnp.einsum('bhtd,bhde->bhte', qi, C) + jnp.einsum('bhts,bhse->bhte', S, vi)
        qn = wi * jnp.einsum('bhtd,bhd->bht', qi, n) + jnp.sum(S, axis=-1)
        h = num / jnp.maximum(jnp.abs(qn), jnp.exp(-mt))[..., None]
        m_new = mt[..., -1]
        wC = jnp.exp(b[..., -1] + m - m_new)
        ws = jnp.exp(b[..., -1:] - b + ii - m_new[..., None])
        C_new = wC[..., None, None] * C + jnp.einsum('bhs,bhsd,bhse->bhde', ws, ki, vi)
        n_new = wC[..., None] * n + jnp.einsum('bhs,bhsd->bhd', ws, ki)
        return (C_new, n_new, m_new), h

    init = (C0.astype(f32), n0.astype(f32), m0.astype(f32))
    (C, n, m), hs = lax.scan(step, init, (qc, kc, vc, igc, lfc))
    h = jnp.swapaxes(jnp.moveaxis(hs, 0, 1), 2, 3).reshape(B, L, H, Dh)
    return h, C, n, m


def trunk_layer(x, lw, fox_attend, sc_past, ml_state, ffn_past, mem_k, mem_v):
    (w_in, b_in, sc_conv_w, ml_norm_g, w_o, ln1_g, ln1_b, w_xq, w_xo, ln2_g, ln2_b,
     w_gate, w_up, ffn_conv_w, w_down, ln3_g, ln3_b) = lw
    B, L, _ = x.shape
    f32 = jnp.float32
    heads = lambda t, nh: t.reshape(B, L, nh, -1)
    z = jnp.einsum('bld,dn->bln', x, w_in) + b_in
    fq, fk, fv, ff, sh, sb, sc, mq, mk, mv, mo, mi, mf = split_in(z)
    fk = heads(fk, FA_HEADS)
    fv = heads(fv, FA_HEADS)
    f_logf = jax.nn.log_sigmoid(ff.astype(f32))
    a_out = fox_attend(heads(fq, FA_HEADS), fk, fv, f_logf).reshape(B, L, FA_W)
    cu, sc_new = causal_dwconv(sc * sh, sc_conv_w, sc_past)
    b_out = sb * cu
    h, C, n, m = mlstm(heads(mq, ML_HEADS), heads(mk, ML_HEADS), heads(mv, ML_HEADS),
                       mi.astype(f32), jax.nn.log_sigmoid(mf.astype(f32)), *ml_state)
    h = jax.nn.sigmoid(heads(mo, ML_HEADS).astype(f32)) * h
    c_out = head_norm(h, ml_norm_g)
    mixed = jnp.concatenate([a_out.astype(x.dtype), b_out.astype(x.dtype), c_out.astype(x.dtype)], axis=-1)
    x = layer_norm(DN_ALPHA * x + mixed @ w_o, ln1_g, ln1_b)
    xq = heads(x @ w_xq, X_HEADS)
    s = jnp.einsum('bqhd,bkhd->bhqk', xq, mem_k, preferred_element_type=f32) * X_HEAD_DIM ** -0.5
    p = jax.nn.softmax(s, axis=-1)
    xo = jnp.einsum('bhqk,bkhd->bqhd', p.astype(mem_v.dtype), mem_v).reshape(B, L, X_W)
    x = layer_norm(DN_ALPHA * x + xo @ w_xo, ln2_g, ln2_b)
    g, ffn_new = causal_dwconv(x @ w_gate, ffn_conv_w, ffn_past)
    y = (jax.nn.silu(g) * (x @ w_up)) @ w_down
    x = layer_norm(DN_ALPHA * x + y, ln3_g, ln3_b)
    return x, (fk, fv, f_logf), sc_new, (C, n, m), ffn_new


def setup_inputs(seed: int = 0) -> dict:
    key = jax.random.key(seed)
    ks = iter(jax.random.split(key, 48))
    nrm = lambda shape, scale=1.0: jax.random.normal(next(ks), shape, jnp.float32) * scale
    n_pages = PAST_LEN // PAGE_SIZE
    n_used = DEC_BATCH * n_pages
    n_phys = n_used + n_used // 4
    page_table = jax.random.permutation(next(ks), n_phys)[:n_used].reshape(DEC_BATCH, n_pages).astype(jnp.int32)
    bnd = [int(v) for v in np.cumsum((0,) + IN_SIZES)]
    gate_off = (jnp.zeros((N_IN,), jnp.float32)
                .at[bnd[3]:bnd[4]].set(FOX_F_BIAS)
                .at[bnd[12]:bnd[13]].set(ML_F_BIAS))
    d = D_MODEL
    inp = {}
    inp['x_prompt'] = nrm((BATCH, SEQ, d))
    inp['x_sample'] = nrm((DEC_BATCH, DEC_SEQ, d))
    inp['cache_fa_k'] = nrm((DEPTH, n_phys, PAGE_SIZE, FA_HEADS, HEAD_DIM))
    inp['cache_fa_v'] = nrm((DEPTH, n_phys, PAGE_SIZE, FA_HEADS, HEAD_DIM))
    inp['cache_fa_logf'] = jax.nn.log_sigmoid(FOX_F_BIAS + nrm((DEPTH, n_phys, PAGE_SIZE, FA_HEADS)))
    inp['cache_mem_k'] = nrm((DEPTH, DEC_BATCH, MEM_TOKENS, X_HEADS, X_HEAD_DIM))
    inp['cache_mem_v'] = nrm((DEPTH, DEC_BATCH, MEM_TOKENS, X_HEADS, X_HEAD_DIM))
    inp['state_sc_conv'] = nrm((DEPTH, DEC_BATCH, SC_WIDTH - 1, SC_W))
    inp['state_ml_C'] = nrm((DEPTH, DEC_BATCH, ML_HEADS, HEAD_DIM, HEAD_DIM), 0.3)
    inp['state_ml_n'] = nrm((DEPTH, DEC_BATCH, ML_HEADS, HEAD_DIM), 0.3)
    inp['state_ml_m'] = nrm((DEPTH, DEC_BATCH, ML_HEADS), 0.5)
    inp['state_ffn_conv'] = nrm((DEPTH, DEC_BATCH, FFN_CONV_WIDTH - 1, D_FF))
    inp['page_table'] = page_table
    inp['mem_prompt'] = nrm((BATCH, MEM_TOKENS, d))
    inp['ln_in_g'] = 1.0 + nrm((d,), 0.02)
    inp['ln_in_b'] = nrm((d,), 0.02)
    inp['w_in'] = nrm((DEPTH, d, N_IN), d ** -0.5)
    inp['b_in'] = nrm((DEPTH, N_IN), 0.02) + gate_off
    inp['sc_conv_w'] = nrm((DEPTH, SC_WIDTH, SC_W), 0.5)
    inp['ml_norm_g'] = 1.0 + nrm((DEPTH, ML_W), 0.02)
    inp['w_o'] = nrm((DEPTH, MIX_W, d), MIX_W ** -0.5 * DN_BETA)
    inp['ln1_g'] = 1.0 + nrm((DEPTH, d), 0.02)
    inp['ln1_b'] = nrm((DEPTH, d), 0.02)
    inp['w_xq'] = nrm((DEPTH, d, X_W), d ** -0.5)
    inp['w_xk'] = nrm((DEPTH, d, X_W), d ** -0.5)
    inp['w_xv'] = nrm((DEPTH, d, X_W), d ** -0.5)
    inp['w_xo'] = nrm((DEPTH, X_W, d), X_W ** -0.5 * DN_BETA)
    inp['ln2_g'] = 1.0 + nrm((DEPTH, d), 0.02)
    inp['ln2_b'] = nrm((DEPTH, d), 0.02)
    inp['w_gate'] = nrm((DEPTH, d, D_FF), d ** -0.5)
    inp['w_up'] = nrm((DEPTH, d, D_FF), d ** -0.5)
    inp['ffn_conv_w'] = nrm((DEPTH, FFN_CONV_WIDTH, D_FF), 0.5)
    inp['w_down'] = nrm((DEPTH, D_FF, d), D_FF ** -0.5 * DN_BETA)
    inp['ln3_g'] = 1.0 + nrm((DEPTH, d), 0.02)
    inp['ln3_b'] = nrm((DEPTH, d), 0.02)
    return inp


def reference(x_prompt, x_sample, cache_fa_k, cache_fa_v, cache_fa_logf, cache_mem_k, cache_mem_v,
              state_sc_conv, state_ml_C, state_ml_n, state_ml_m, state_ffn_conv, page_table, mem_prompt,
              ln_in_g, ln_in_b, w_in, b_in, sc_conv_w, ml_norm_g, w_o, ln1_g, ln1_b,
              w_xq, w_xk, w_xv, w_xo, ln2_g, ln2_b, w_gate, w_up, ffn_conv_w, w_down, ln3_g, ln3_b):
    f32 = jnp.float32
    xp = layer_norm(x_prompt, ln_in_g, ln_in_b)
    xs = layer_norm(x_sample, ln_in_g, ln_in_b)
    bp = xp.shape[0]
    n_mem = mem_prompt.shape[1]
    P = {n: [] for n in ('fa_k', 'fa_v', 'fa_lf', 'mem_k', 'mem_v', 'sc', 'mC', 'mn', 'mm', 'ffn')}
    Q = {n: [] for n in ('fa_k', 'fa_v', 'fa_lf', 'sc', 'mC', 'mn', 'mm', 'ffn')}
    for l in range(DEPTH):
        lw = (w_in[l], b_in[l], sc_conv_w[l], ml_norm_g[l], w_o[l], ln1_g[l], ln1_b[l], w_xq[l], w_xo[l],
              ln2_g[l], ln2_b[l], w_gate[l], w_up[l], ffn_conv_w[l], w_down[l], ln3_g[l], ln3_b[l])
        mem_k = (mem_prompt @ w_xk[l]).reshape(bp, n_mem, X_HEADS, X_HEAD_DIM)
        mem_v = (mem_prompt @ w_xv[l]).reshape(bp, n_mem, X_HEADS, X_HEAD_DIM)
        ml0 = (jnp.zeros((bp, ML_HEADS, HEAD_DIM, HEAD_DIM), f32), jnp.zeros((bp, ML_HEADS, HEAD_DIM), f32),
               jnp.zeros((bp, ML_HEADS), f32))
        xp, fa, sc, ml, ffn = trunk_layer(xp, lw, fox_prompt, jnp.zeros((bp, SC_WIDTH - 1, SC_W), xp.dtype), ml0,
                                          jnp.zeros((bp, FFN_CONV_WIDTH - 1, D_FF), xp.dtype), mem_k, mem_v)
        for n_, v_ in zip(('fa_k', 'fa_v', 'fa_lf', 'mem_k', 'mem_v', 'sc', 'mC', 'mn', 'mm', 'ffn'),
                          (fa[0], fa[1], fa[2], mem_k, mem_v, sc, ml[0], ml[1], ml[2], ffn)):
            P[n_].append(v_)
        fox_s = functools.partial(fox_sample, k_pool=cache_fa_k[l], v_pool=cache_fa_v[l],
                                  lf_pool=cache_fa_logf[l], page_table=page_table)
        xs, fa, sc, ml, ffn = trunk_layer(xs, lw, fox_s, state_sc_conv[l],
                                          (state_ml_C[l], state_ml_n[l], state_ml_m[l]), state_ffn_conv[l],
                                          cache_mem_k[l], cache_mem_v[l])
        for n_, v_ in zip(('fa_k', 'fa_v', 'fa_lf', 'sc', 'mC', 'mn', 'mm', 'ffn'),
                          (fa[0], fa[1], fa[2], sc, ml[0], ml[1], ml[2], ffn)):
            Q[n_].append(v_)
    return (xp, xs,
            jnp.stack(P['fa_k']), jnp.stack(P['fa_v']), jnp.stack(P['fa_lf']),
            jnp.stack(P['mem_k']), jnp.stack(P['mem_v']), jnp.stack(P['sc']),
            jnp.stack(P['mC']), jnp.stack(P['mn']), jnp.stack(P['mm']), jnp.stack(P['ffn']),
            jnp.stack(Q['fa_k']), jnp.stack(Q['fa_v']), jnp.stack(Q['fa_lf']), jnp.stack(Q['sc']),
            jnp.stack(Q['mC']), jnp.stack(Q['mn']), jnp.stack(Q['mm']), jnp.stack(Q['ffn']))
```

```python
import functools

import jax
import jax.numpy as jnp
from jax import lax
from jax.experimental import pallas as pl
from jax.experimental.pallas import tpu as pltpu

F32 = jnp.float32
BF16 = jnp.bfloat16

HEAD_DIM = 64
FA_HEADS = 8
FA_W = FA_HEADS * HEAD_DIM
SC_W = 256
ML_HEADS = 4
ML_W = ML_HEADS * HEAD_DIM
X_HEADS = 4
LN_EPS = 1e-5
DEPTH_FOR_ALPHA = 4
DN_ALPHA = (2.0 * DEPTH_FOR_ALPHA) ** 0.25
NEG_BIG = -1e30

C_Q, C_K, C_V = 0, 512, 1024
C_CV = 1536
C_ML = 2304
C_G = 3328
N_INP = 3456
G_LF, G_MI, G_MF = 0, 8, 12

VMEM_LIMIT = 56 * 1024 * 1024


def _cparams(sem):
    return pltpu.CompilerParams(dimension_semantics=sem, vmem_limit_bytes=VMEM_LIMIT)


def _resident(shape):
    nd = len(shape)
    return pl.BlockSpec(shape, lambda *_: (0,) * nd, pipeline_mode=pl.Buffered(1))


def _iota(shape, dim):
    return lax.broadcasted_iota(jnp.int32, shape, dim)


def _split3(a):
    a1 = a.astype(BF16)
    r = a - a1.astype(F32)
    a2 = r.astype(BF16)
    r = r - a2.astype(F32)
    return a1, a2, r.astype(BF16)


def _dot01_right(a, m01):
    return sum(jnp.dot(p, m01, preferred_element_type=F32) for p in _split3(a))


def _dot01_left(m01, a):
    return sum(jnp.dot(m01, p, preferred_element_type=F32) for p in _split3(a))


def _dot_nt(a, b):
    return lax.dot_general(a, b, (((1,), (1,)), ((), ())), preferred_element_type=F32)


def _ln(x, g, b):
    mu = jnp.mean(x, axis=-1, keepdims=True)
    xc = x - mu
    var = jnp.mean(xc * xc, axis=-1, keepdims=True)
    return xc * lax.rsqrt(var + LN_EPS) * g + b


def _log_sigmoid(x):
    return jnp.minimum(x, 0.0) - jnp.log1p(jnp.exp(-jnp.abs(x)))


def _sigmoid(x):
    return 1.0 / (1.0 + jnp.exp(-x))


def _conv3(g, w_ref, sl, h1, h2, m1, m2):
    g1 = jnp.where(m1, h1, pltpu.roll(g, 1, 0))
    g2 = jnp.where(m2, h2, pltpu.roll(g, 2, 0))
    return w_ref[0:1, sl] * g2 + w_ref[1:2, sl] * g1 + w_ref[2:3, sl] * g


def _halo_from_carry(prev, first, shape):
    prev = jnp.where(first, 0.0, prev)
    row = _iota(shape, 0)
    p6 = jnp.broadcast_to(prev[6:7, :], shape)
    p7 = jnp.broadcast_to(prev[7:8, :], shape)
    return p7, jnp.where(row == 0, p6, p7), row == 0, row < 2


def _ln_kernel(x_ref, g_ref, b_ref, o_ref):
    o_ref[...] = _ln(x_ref[...], g_ref[...], b_ref[...])


def _layer_norm(x, g, b, tm):
    m, d = x.shape
    return pl.pallas_call(
        _ln_kernel, out_shape=jax.ShapeDtypeStruct((m, d), F32), grid=(m // tm,),
        in_specs=[pl.BlockSpec((tm, d), lambda i: (i, 0)), _resident((1, d)), _resident((1, d))],
        out_specs=pl.BlockSpec((tm, d), lambda i: (i, 0)),
        compiler_params=_cparams(("parallel",)), name="ln_in")(x, g, b)


def _inproj_kernel(x_ref, w_ref, b_ref, q_ref, k_ref, v_ref, kb_ref, vb_ref, cv_ref, ml_ref, g_ref):
    xb = x_ref[...].astype(BF16)

    def mm(lo, hi):
        return jnp.dot(xb, w_ref[:, lo:hi], preferred_element_type=F32) + b_ref[:, lo:hi]

    for c in range(2):
        sl = slice(c * 256, (c + 1) * 256)
        q_ref[:, sl] = (mm(C_Q + c * 256, C_Q + (c + 1) * 256) * HEAD_DIM ** -0.5).astype(BF16)
        k = mm(C_K + c * 256, C_K + (c + 1) * 256)
        k_ref[:, sl] = k
        kb_ref[:, sl] = k.astype(BF16)
        v = mm(C_V + c * 256, C_V + (c + 1) * 256)
        v_ref[:, sl] = v
        vb_ref[:, sl] = v.astype(BF16)
    for c in range(3):
        cv_ref[:, c * 256:(c + 1) * 256] = mm(C_CV + c * 256, C_CV + (c + 1) * 256)
    for c in range(4):
        r = mm(C_ML + c * 256, C_ML + (c + 1) * 256)
        if c == 1:
            r = r * HEAD_DIM ** -0.5
        ml_ref[:, c * 256:(c + 1) * 256] = r
    g = mm(C_G, C_G + 128)
    lane = _iota(g.shape, 1)
    is_ls = (lane < G_MI) | ((lane >= G_MF) & (lane < G_MF + ML_HEADS))
    g_ref[...] = jnp.where(is_ls, _log_sigmoid(g), g)


def _inproj(x, w, b, tm):
    m, d = x.shape
    row = lambda n: pl.BlockSpec((tm, n), lambda i: (i, 0))
    outs = [((m, FA_W), BF16), ((m, FA_W), F32), ((m, FA_W), F32), ((m, FA_W), BF16), ((m, FA_W), BF16),
            ((m, 3 * SC_W), F32), ((m, 4 * ML_W), F32), ((m, 128), F32)]
    return pl.pallas_call(
        _inproj_kernel,
        out_shape=[jax.ShapeDtypeStruct(s, t) for s, t in outs],
        grid=(m // tm,),
        in_specs=[row(d), _resident((d, N_INP)), _resident((1, N_INP))],
        out_specs=[row(s[1]) for s, _ in outs],
        compiler_params=_cparams(("parallel",)), name="inproj")(x, w, b)


def _cumsum_kernel(g_ref, o_ref, *, nchunk):
    u01 = (_iota((128, 128), 0) <= _iota((128, 128), 1)).astype(BF16)

    def body(i, carry):
        sl = pl.ds(pl.multiple_of(i * 128, 128), 128)
        y = _dot01_right(g_ref[0, :, sl], u01) + carry
        o_ref[0, :, sl] = y
        return y[:, 127:128]

    lax.fori_loop(0, nchunk, body, jnp.zeros((g_ref.shape[1], 1), F32))


def _cumsum_rows(g):
    b, r, s = g.shape
    return pl.pallas_call(
        functools.partial(_cumsum_kernel, nchunk=s // 128),
        out_shape=jax.ShapeDtypeStruct((b, r, s), F32), grid=(b,),
        in_specs=[pl.BlockSpec((1, r, s), lambda i: (i, 0, 0))],
        out_specs=pl.BlockSpec((1, r, s), lambda i: (i, 0, 0)),
        compiler_params=_cparams(("parallel",)), name="logf_cumsum")(g)


def _fox_prompt_kernel(q_ref, k_ref, v_ref, f_ref, o_ref, m_sc, l_sc, acc_sc):
    i, j = pl.program_id(2), pl.program_id(3)
    tq, tk = q_ref.shape[0], k_ref.shape[0]

    @pl.when(j == 0)
    def _():
        m_sc[...] = jnp.full(m_sc.shape, -jnp.inf, F32)
        l_sc[...] = jnp.zeros(l_sc.shape, F32)
        acc_sc[...] = jnp.zeros(acc_sc.shape, F32)

    def step(masked):
        q, k, v = q_ref[...], k_ref[...], v_ref[...]
        lane_head = _iota((1, 128), 1) >> 6
        for hh in range(2):
            qm = jnp.where(lane_head == hh, q, jnp.zeros_like(q))
            s = _dot_nt(qm, k) - f_ref[0, hh:hh + 1, :]
            if masked:
                s = jnp.where(_iota((tq, tk), 1) <= _iota((tq, tk), 0), s, -jnp.inf)
            m_prev = m_sc[hh]
            m_new = jnp.maximum(m_prev, jnp.max(s, axis=1, keepdims=True))
            alpha = jnp.exp(m_prev - m_new)
            p = jnp.exp(s - m_new)
            l_sc[hh] = alpha * l_sc[hh] + jnp.sum(p, axis=1, keepdims=True)
            acc_sc[hh] = alpha * acc_sc[hh] + jnp.dot(p.astype(BF16), v, preferred_element_type=F32)
            m_sc[hh] = m_new

    @pl.when(j < i)
    def _():
        step(False)

    @pl.when(j == i)
    def _():
        step(True)
        lane_head = _iota((1, 128), 1) >> 6
        o = jnp.where(lane_head == 0, acc_sc[0] / l_sc[0], acc_sc[1] / l_sc[1])
        o_ref[...] = o.astype(o_ref.dtype)


def _fox_prompt(q, k, v, frow, bsz, seq, tq):
    m = q.shape[0]
    nq = seq // tq
    qspec = pl.BlockSpec((tq, 128), lambda b, hp, i, j: (b * nq + i, hp))
    kspec = pl.BlockSpec((tq, 128), lambda b, hp, i, j: (b * nq + jnp.minimum(j, i), hp))
    fspec = pl.BlockSpec((None, 1, 2, tq), lambda b, hp, i, j: (b, hp, 0, jnp.minimum(j, i)))
    return pl.pallas_call(
        _fox_prompt_kernel,
        out_shape=jax.ShapeDtypeStruct((m, FA_W), BF16),
        grid=(bsz, FA_HEADS // 2, nq, nq),
        in_specs=[qspec, kspec, kspec, fspec],
        out_specs=qspec,
        scratch_shapes=[pltpu.VMEM((2, tq, 1), F32), pltpu.VMEM((2, tq, 1), F32), pltpu.VMEM((2, tq, 128), F32)],
        compiler_params=_cparams(("parallel", "parallel", "parallel", "arbitrary")),
        name="fox_prompt")(q, k, v, frow)


def _fox_sample_kernel(pt_ref, q_ref, kn_ref, vn_ref, ln_ref, *rest, pg):
    kp, vp, lp = rest[0:pg], rest[pg:2 * pg], rest[2 * pg:3 * pg]
    o_ref = rest[3 * pg]
    m_sc, l_sc, acc_sc, g_sc = rest[3 * pg + 1:]
    s_id, ns = pl.program_id(1), pl.num_programs(1)
    nrow = FA_HEADS * 8

    @pl.when(s_id == 0)
    def _():
        m_sc[...] = jnp.full(m_sc.shape, -jnp.inf, F32)
        l_sc[...] = jnp.zeros(l_sc.shape, F32)
        acc_sc[...] = jnp.zeros(acc_sc.shape, F32)
        g_sc[...] = jnp.zeros(g_sc.shape, F32)

    q = q_ref[0]
    qe = jnp.concatenate([q] * FA_HEADS, axis=0)
    head_match = (_iota((nrow, FA_W), 0) >> 3) == (_iota((nrow, FA_W), 1) >> 6)
    qe = jnp.where(head_match, qe, 0.0).astype(BF16)
    u01 = (_iota((128, 128), 0) <= _iota((128, 128), 1)).astype(BF16)

    def page(kf, vf, lrow, mask):
        kb, vb = kf.astype(BF16), vf.astype(BF16)
        g = _dot01_right(lrow, u01) + g_sc[...]
        ge = jnp.concatenate([jnp.broadcast_to(g[h:h + 1, :], (8, 128)) for h in range(FA_HEADS)], axis=0)
        s = _dot_nt(qe, kb) - ge
        if mask is not None:
            s = jnp.where(mask, s, -jnp.inf)
        m_prev = m_sc[...]
        m_new = jnp.maximum(m_prev, jnp.max(s, axis=1, keepdims=True))
        alpha = jnp.exp(m_prev - m_new)
        p = jnp.exp(s - m_new)
        l_sc[...] = alpha * l_sc[...] + jnp.sum(p, axis=1, keepdims=True)
        acc_sc[...] = alpha * acc_sc[...] + jnp.dot(p.astype(BF16), vb, preferred_element_type=F32)
        m_sc[...] = m_new
        g_sc[...] = g[:, 127:128]

    for p in range(pg):
        page(kp[p][0], vp[p][0], lp[p][0], None)

    @pl.when(s_id == ns - 1)
    def _():
        causal = _iota((nrow, 128), 1) <= (_iota((nrow, 128), 0) & 7)
        page(kn_ref[0], vn_ref[0], ln_ref[0], causal)
        o = acc_sc[...] / l_sc[...]
        lane_head = _iota((8, FA_W), 1) >> 6
        out = jnp.zeros((8, FA_W), F32)
        for h in range(FA_HEADS):
            out = jnp.where(lane_head == h, o[h * 8:(h + 1) * 8, :], out)
        o_ref[0] = out.astype(o_ref.dtype)


def _fox_sample(page_table, q, k_new, v_new, lf_new, k_pool, v_pool, lf_pool, pg):
    db, n_pages = page_table.shape
    page = lambda shp, p: pl.BlockSpec(shp, lambda b, s, pt: (pt[b, s * pg + p], 0, 0))
    per_seq = lambda shp: pl.BlockSpec(shp, lambda b, s, pt: (b, 0, 0))
    in_specs = ([per_seq((1, 8, FA_W)), per_seq((1, 128, FA_W)), per_seq((1, 128, FA_W)), per_seq((1, 8, 128))]
                + [page((1, 128, FA_W), p) for p in range(pg)]
                + [page((1, 128, FA_W), p) for p in range(pg)]
                + [page((1, 8, 128), p) for p in range(pg)])
    nrow = FA_HEADS * 8
    return pl.pallas_call(
        functools.partial(_fox_sample_kernel, pg=pg),
        out_shape=jax.ShapeDtypeStruct((db, 8, FA_W), F32),
        grid_spec=pltpu.PrefetchScalarGridSpec(
            num_scalar_prefetch=1, grid=(db, n_pages // pg), in_specs=in_specs,
            out_specs=per_seq((1, 8, FA_W)),
            scratch_shapes=[pltpu.VMEM((nrow, 1), F32), pltpu.VMEM((nrow, 1), F32),
                            pltpu.VMEM((nrow, FA_W), F32), pltpu.VMEM((FA_HEADS, 1), F32)]),
        compiler_params=_cparams(("parallel", "arbitrary")),
        name="fox_sample")(page_table, q, k_new, v_new, lf_new, *([k_pool] * pg), *([v_pool] * pg), *([lf_pool] * pg))


def _mlstm_kernel(q_ref, k_ref, v_ref, gc_ref, gr_ref, c0_ref, n0_ref, m0_ref, h_ref, c_ref, n_ref, m_ref):
    c = q_ref.shape[0]

    @pl.when(pl.program_id(1) == 0)
    def _():
        c_ref[...] = c0_ref[...]
        n_ref[...] = n0_ref[...]
        m_ref[...] = m0_ref[...]

    q, k, v = q_ref[...], k_ref[...], v_ref[...]
    qb, kb, vb = q.astype(BF16), k.astype(BF16), v.astype(BF16)
    gc, gr = gc_ref[...], gr_ref[...]
    row, col = _iota((c, c), 0), _iota((c, c), 1)
    tri = col <= row
    bc_all = _dot01_left(tri.astype(BF16), gc)
    br_all = _dot01_right(gr, (row <= col).astype(BF16))
    cbd, nm, mrow = c_ref[0], n_ref[0], m_ref[0]
    q_c = jnp.dot(qb, cbd.astype(BF16), preferred_element_type=F32)
    q_n = jnp.dot(qb, nm.astype(BF16), preferred_element_type=F32)
    lane_head = _iota((1, ML_W), 1) >> 6
    lane128 = _iota((1, 128), 1)
    h_acc = jnp.zeros((c, ML_W), F32)
    kw = jnp.zeros((c, ML_W), F32)
    wc_lane = jnp.zeros((1, ML_W), F32)
    wc128 = jnp.zeros((1, 128), F32)
    m_next = jnp.zeros((1, 128), F32)
    ws_cols = jnp.zeros((c, 128), F32)
    for hh in range(ML_HEADS):
        lm = lane_head == hh
        b_col = bc_all[:, G_MF + hh:G_MF + hh + 1]
        i_col = gc[:, G_MI + hh:G_MI + hh + 1]
        b_row = br_all[G_MF + hh:G_MF + hh + 1, :]
        i_row = gr[G_MI + hh:G_MI + hh + 1, :]
        m_h = mrow[:, hh:hh + 1]
        a_col = b_col + m_h
        dmat = jnp.where(tri, b_col - b_row + i_row, -jnp.inf)
        mt = jnp.maximum(a_col, jnp.max(dmat, axis=1, keepdims=True))
        wi = jnp.exp(a_col - mt)
        smat = _dot_nt(jnp.where(lm, qb, jnp.zeros_like(qb)), kb) * jnp.exp(dmat - mt)
        s_v = jnp.dot(smat.astype(BF16), vb, preferred_element_type=F32)
        qn = wi * q_n[:, hh:hh + 1] + jnp.sum(smat, axis=1, keepdims=True)
        den = jnp.maximum(jnp.abs(qn), jnp.exp(-mt))
        h_acc = jnp.where(lm, (wi * q_c + s_v) / den, h_acc)
        m_new = mt[c - 1:c, :]
        b_last = b_col[c - 1:c, :]
        wc = jnp.exp(b_last + m_h - m_new)
        ws = jnp.exp(b_last - b_col + i_col - m_new)
        kw = jnp.where(lm, k * ws, kw)
        wc_lane = jnp.where(lm, wc, wc_lane)
        sel = lane128 == hh
        wc128 = jnp.where(sel, wc, wc128)
        m_next = jnp.where(sel, m_new, m_next)
        ws_cols = jnp.where(sel, ws, ws_cols)
    h_ref[...] = h_acc
    d_c = jnp.dot(kw.T.astype(BF16), vb, preferred_element_type=F32)
    same_head = (_iota((ML_W, ML_W), 0) >> 6) == (_iota((ML_W, ML_W), 1) >> 6)
    c_ref[0] = cbd * wc_lane + jnp.where(same_head, d_c, 0.0)
    d_n = jnp.dot(k.T.astype(BF16), ws_cols.astype(BF16), preferred_element_type=F32)
    own_col = (_iota((ML_W, 128), 0) >> 6) == _iota((ML_W, 128), 1)
    n_ref[0] = nm * wc128 + jnp.where(own_col, d_n, 0.0)
    m_ref[0] = m_next


def _mlstm(ml, gcol, grow, c0, n0, m0, bsz, nchunk, c, qkv_blocked):
    m = bsz * nchunk * c
    rows = lambda blk: pl.BlockSpec((c, ML_W), lambda b, j: (b * nchunk + j, blk))
    if qkv_blocked:
        qkv, qkv_specs = (ml, ml, ml), [rows(0), rows(1), rows(2)]
    else:
        qkv, qkv_specs = ml, [rows(0), rows(0), rows(0)]
    state = lambda shp: pl.BlockSpec(shp, lambda b, j: (b, 0, 0))
    return pl.pallas_call(
        _mlstm_kernel,
        out_shape=[jax.ShapeDtypeStruct((m, ML_W), F32), jax.ShapeDtypeStruct((bsz, ML_W, ML_W), F32),
                   jax.ShapeDtypeStruct((bsz, ML_W, 128), F32), jax.ShapeDtypeStruct((bsz, 1, 128), F32)],
        grid=(bsz, nchunk),
        in_specs=qkv_specs + [pl.BlockSpec((c, 128), lambda b, j: (b * nchunk + j, 0)),
                              pl.BlockSpec((16, c), lambda b, j: (0, b * nchunk + j)),
                              state((1, ML_W, ML_W)), state((1, ML_W, 128)), state((1, 1, 128))],
        out_specs=[pl.BlockSpec((c, ML_W), lambda b, j: (b * nchunk + j, 0)),
                   state((1, ML_W, ML_W)), state((1, ML_W, 128)), state((1, 1, 128))],
        compiler_params=_cparams(("parallel", "arbitrary")), name="mlstm")(*qkv, gcol, grow, c0, n0, m0)


def _mixproj_kernel(*refs, tps, sample):
    if sample:
        (a_ref, cv_ref, h_ref, mo_ref, x_ref, wo_ref, cw_ref, ng_ref, lg_ref, lb_ref, h1_ref, h2_ref,
         o_ref, tail_ref) = refs
    else:
        (a_ref, cv_ref, h_ref, mo_ref, x_ref, wo_ref, cw_ref, ng_ref, lg_ref, lb_ref,
         o_ref, tail_ref, carry_sc) = refs
    tm = x_ref.shape[0]
    u = cv_ref[:, 2 * SC_W:3 * SC_W] * cv_ref[:, 0:SC_W]
    if sample:
        rmod = _iota(u.shape, 0) & 7
        h1, h2, m1, m2 = h1_ref[...], h2_ref[...], rmod == 0, rmod < 2
        tail_ref[...] = u
    else:
        first = (pl.program_id(0) % tps) == 0
        h1, h2, m1, m2 = _halo_from_carry(carry_sc[...], first, u.shape)
        carry_sc[...] = u[tm - 8:tm, :]
        tail_ref[...] = u[tm - 8:tm, :]
    b_out = cv_ref[:, SC_W:2 * SC_W] * _conv3(u, cw_ref, slice(0, SC_W), h1, h2, m1, m2)
    hg = _sigmoid(mo_ref[...]) * h_ref[...]
    seg = ((_iota((ML_W, ML_W), 0) >> 6) == (_iota((ML_W, ML_W), 1) >> 6)).astype(BF16)
    mu = _dot01_right(hg, seg) * (1.0 / HEAD_DIM)
    hc = hg - mu
    var = _dot01_right(hc * hc, seg) * (1.0 / HEAD_DIM)
    c_out = hc * lax.rsqrt(var + LN_EPS) * ng_ref[...]
    y = (jnp.dot(a_ref[...].astype(BF16), wo_ref[0:FA_W, :], preferred_element_type=F32)
         + jnp.dot(b_out.astype(BF16), wo_ref[FA_W:FA_W + SC_W, :], preferred_element_type=F32)
         + jnp.dot(c_out.astype(BF16), wo_ref[FA_W + SC_W:, :], preferred_element_type=F32))
    o_ref[...] = _ln(DN_ALPHA * x_ref[...] + y, lg_ref[...], lb_ref[...])


def _mixproj(a_out, cv, h, ml, x, w_o, conv_w, norm_g, ln_g, ln_b, tm, tps, halo):
    m, d = x.shape
    sample = halo is not None
    row = lambda n, blk=0: pl.BlockSpec((tm, n), lambda i: (i, blk))
    in_specs = [row(FA_W), row(3 * SC_W), row(ML_W), row(ML_W, 3), row(d), _resident(w_o.shape),
                _resident(conv_w.shape), _resident((1, ML_W)), _resident((1, d)), _resident((1, d))]
    args = [a_out, cv, h, ml, x, w_o, conv_w, norm_g, ln_g, ln_b]
    if sample:
        in_specs += [row(SC_W), row(SC_W)]
        args += list(halo)
        tail_shape, tail_spec, scratch = (m, SC_W), row(SC_W), []
    else:
        nseq = m // (tm * tps)
        tail_shape, tail_spec = (nseq * 8, SC_W), pl.BlockSpec((8, SC_W), lambda i: (i // tps, 0))
        scratch = [pltpu.VMEM((8, SC_W), F32)]
    return pl.pallas_call(
        functools.partial(_mixproj_kernel, tps=tps, sample=sample),
        out_shape=[jax.ShapeDtypeStruct((m, d), F32), jax.ShapeDtypeStruct(tail_shape, F32)],
        grid=(m // tm,), in_specs=in_specs, out_specs=[row(d), tail_spec], scratch_shapes=scratch,
        compiler_params=_cparams(("arbitrary",)), name="mixproj")(*args)


def _dense_kernel(x_ref, w_ref, o_ref):
    o_ref[...] = jnp.dot(x_ref[...].astype(BF16), w_ref[...], preferred_element_type=F32).astype(o_ref.dtype)


def _dense(x, w, out_dtype, tm):
    m, d = x.shape
    n = w.shape[1]
    return pl.pallas_call(
        _dense_kernel, out_shape=jax.ShapeDtypeStruct((m, n), out_dtype), grid=(m // tm,),
        in_specs=[pl.BlockSpec((tm, d), lambda i: (i, 0)), _resident(w.shape)],
        out_specs=pl.BlockSpec((tm, n), lambda i: (i, 0)),
        compiler_params=_cparams(("parallel",)), name="dense")(x, w)


def _dense_ln_kernel(a_ref, w_ref, r_ref, g_ref, b_ref, o_ref):
    y = jnp.dot(a_ref[...].astype(BF16), w_ref[...], preferred_element_type=F32)
    o_ref[...] = _ln(DN_ALPHA * r_ref[...] + y, g_ref[...], b_ref[...])


def _dense_ln(a, w, resid, g, b, tm):
    m, d = resid.shape
    row = lambda n: pl.BlockSpec((tm, n), lambda i: (i, 0))
    return pl.pallas_call(
        _dense_ln_kernel, out_shape=jax.ShapeDtypeStruct((m, d), F32), grid=(m // tm,),
        in_specs=[row(a.shape[1]), _resident(w.shape), row(d), _resident((1, d)), _resident((1, d))],
        out_specs=row(d), compiler_params=_cparams(("parallel",)), name="dense_ln")(a, w, resid, g, b)


def _xattn_kernel(q_ref, k_ref, v_ref, o_ref):
    hd = q_ref.shape[1] // X_HEADS
    for h in range(X_HEADS):
        sl = slice(h * hd, (h + 1) * hd)
        k = k_ref[0, :, sl].astype(BF16)
        v = v_ref[0, :, sl].astype(BF16)
        s = _dot_nt(q_ref[:, sl].astype(BF16), k) * hd ** -0.5
        e = jnp.exp(s - jnp.max(s, axis=1, keepdims=True))
        p = e / jnp.sum(e, axis=1, keepdims=True)
        o_ref[:, sl] = jnp.dot(p.astype(BF16), v, preferred_element_type=F32).astype(o_ref.dtype)


def _xattn(q, mem_k, mem_v, tm, tiles_per_seq):
    m, d = q.shape
    nmem = mem_k.shape[1]
    mem = pl.BlockSpec((1, nmem, d), lambda i: (i // tiles_per_seq, 0, 0))
    return pl.pallas_call(
        _xattn_kernel, out_shape=jax.ShapeDtypeStruct((m, d), q.dtype), grid=(m // tm,),
        in_specs=[pl.BlockSpec((tm, d), lambda i: (i, 0)), mem, mem],
        out_specs=pl.BlockSpec((tm, d), lambda i: (i, 0)),
        compiler_params=_cparams(("parallel",)), name="xattn")(q, mem_k, mem_v)


def _ffn_kernel(*refs, tps, sample, ck):
    if sample:
        x_ref, wg_ref, wu_ref, wd_ref, cw_ref, lg_ref, lb_ref, h1_ref, h2_ref, o_ref, tail_ref, acc_sc = refs
    else:
        x_ref, wg_ref, wu_ref, wd_ref, cw_ref, lg_ref, lb_ref, o_ref, tail_ref, acc_sc, carry_sc = refs
    tm = x_ref.shape[0]
    x = x_ref[...]
    xb = x.astype(BF16)
    first = None if sample else (pl.program_id(0) % tps) == 0
    rmod = _iota((tm, ck), 0) & 7
    for c in range(wg_ref.shape[1] // ck):
        sl = slice(c * ck, (c + 1) * ck)
        g = jnp.dot(xb, wg_ref[:, sl], preferred_element_type=F32)
        up = jnp.dot(xb, wu_ref[:, sl], preferred_element_type=F32)
        if sample:
            h1, h2, m1, m2 = h1_ref[:, sl], h2_ref[:, sl], rmod == 0, rmod < 2
            tail_ref[:, sl] = g
        else:
            h1, h2, m1, m2 = _halo_from_carry(carry_sc[:, sl], first, g.shape)
            carry_sc[:, sl] = g[tm - 8:tm, :]
            tail_ref[:, sl] = g[tm - 8:tm, :]
        gc = _conv3(g, cw_ref, sl, h1, h2, m1, m2)
        act = (gc * _sigmoid(gc) * up).astype(BF16)
        part = jnp.dot(act, wd_ref[sl, :], preferred_element_type=F32)
        if c == 0:
            acc_sc[...] = part
        else:
            acc_sc[...] += part
    o_ref[...] = _ln(DN_ALPHA * x + acc_sc[...], lg_ref[...], lb_ref[...])


def _ffn(x, w_gate, w_up, w_down, conv_w, ln_g, ln_b, tm, tps, halo):
    m, d = x.shape
    dff = w_gate.shape[1]
    sample = halo is not None
    row = lambda n: pl.BlockSpec((tm, n), lambda i: (i, 0))
    in_specs = [row(d), _resident(w_gate.shape), _resident(w_up.shape), _resident(w_down.shape),
                _resident(conv_w.shape), _resident((1, d)), _resident((1, d))]
    args = [x, w_gate, w_up, w_down, conv_w, ln_g, ln_b]
    scratch = [pltpu.VMEM((tm, d), F32)]
    if sample:
        in_specs += [row(dff), row(dff)]
        args += list(halo)
        tail_shape, tail_spec = (m, dff), row(dff)
    else:
        nseq = m // (tm * tps)
        tail_shape, tail_spec = (nseq * 8, dff), pl.BlockSpec((8, dff), lambda i: (i // tps, 0))
        scratch.append(pltpu.VMEM((8, dff), F32))
    return pl.pallas_call(
        functools.partial(_ffn_kernel, tps=tps, sample=sample, ck=256),
        out_shape=[jax.ShapeDtypeStruct((m, d), F32), jax.ShapeDtypeStruct(tail_shape, F32)],
        grid=(m // tm,), in_specs=in_specs, out_specs=[row(d), tail_spec], scratch_shapes=scratch,
        compiler_params=_cparams(("arbitrary",)), name="ffn")(*args)


def _permute_w_in(w_in, b_in):
    o_ff = 3 * FA_W
    o_cv = o_ff + FA_HEADS
    o_ml = o_cv + 3 * SC_W
    o_mi = o_ml + 4 * ML_W
    n_in = o_mi + 2 * ML_HEADS

    def perm(a):
        pad = jnp.zeros(a.shape[:-1] + (128 - FA_HEADS - 2 * ML_HEADS,), a.dtype)
        return jnp.concatenate([a[..., 0:o_ff], a[..., o_cv:o_ml], a[..., o_ml:o_mi],
                                a[..., o_ff:o_cv], a[..., o_mi:n_in], pad], axis=-1)

    return perm(w_in).astype(BF16), perm(b_in)[:, None, :]


def _block_diag_state(c, n, mm):
    bsz = c.shape[0]
    eye = jnp.eye(ML_HEADS, dtype=c.dtype)
    cbd = (c[:, :, :, None, :] * eye[None, :, None, :, None]).reshape(bsz, ML_W, ML_W)
    nm = (n[:, :, :, None] * eye[None, :, None, :]).reshape(bsz, ML_W, ML_HEADS)
    nm = jnp.pad(nm, ((0, 0), (0, 0), (0, 128 - ML_HEADS)))
    return cbd, nm, jnp.pad(mm, ((0, 0), (0, 128 - ML_HEADS)))[:, None, :]


def _unpack_state(cbd, nm, mrow):
    bsz = cbd.shape[0]
    c5 = cbd.reshape(bsz, ML_HEADS, HEAD_DIM, ML_HEADS, HEAD_DIM)
    c = jnp.stack([c5[:, h, :, h, :] for h in range(ML_HEADS)], axis=1)
    n3 = nm[:, :, :ML_HEADS].reshape(bsz, ML_HEADS, HEAD_DIM, ML_HEADS)
    n = jnp.stack([n3[:, h, :, h] for h in range(ML_HEADS)], axis=1)
    return c, n, mrow[:, 0, :ML_HEADS]


def _tail_halo(past):
    db, _, ch = past.shape
    z = jnp.zeros((db, 8, ch), past.dtype)
    h1 = z.at[:, 0].set(past[:, 1])
    h2 = z.at[:, 0].set(past[:, 0]).at[:, 1].set(past[:, 1])
    return h1.reshape(db * 8, ch), h2.reshape(db * 8, ch)


def kernel(x_prompt, x_sample, cache_fa_k, cache_fa_v, cache_fa_logf, cache_mem_k, cache_mem_v, state_sc_conv,
           state_ml_C, state_ml_n, state_ml_m, state_ffn_conv, page_table, mem_prompt, ln_in_g, ln_in_b, w_in, b_in,
           sc_conv_w, ml_norm_g, w_o, ln1_g, ln1_b, w_xq, w_xk, w_xv, w_xo, ln2_g, ln2_b, w_gate, w_up, ffn_conv_w,
           w_down, ln3_g, ln3_b):
    bsz, seq, d = x_prompt.shape
    db, dseq, _ = x_sample.shape
    depth = w_in.shape[0]
    n_phys, page_size = cache_fa_k.shape[1], cache_fa_k.shape[2]
    nmem = mem_prompt.shape[1]
    dff = w_gate.shape[2]
    assert dseq == 8 and page_size == 128 and d == 1024 and depth == DEPTH_FOR_ALPHA
    mp, ms = bsz * seq, db * dseq
    tm = 512 if seq % 512 == 0 and seq > 512 else 256
    tps = seq // tm
    tq = tm
    ml_chunk = 128
    pg = 8

    w_in_p, b_in_p = _permute_w_in(w_in, b_in)
    bf = lambda a: a.astype(BF16)
    w_o_b, w_xq_b, w_xk_b, w_xv_b, w_xo_b = bf(w_o), bf(w_xq), bf(w_xk), bf(w_xv), bf(w_xo)
    w_gate_b, w_up_b, w_down_b = bf(w_gate), bf(w_up), bf(w_down)
    vec = lambda a: a[:, None, :]
    ml_norm_g, ln1_g, ln1_b, ln2_g, ln2_b, ln3_g, ln3_b = map(vec, (ml_norm_g, ln1_g, ln1_b, ln2_g, ln2_b, ln3_g, ln3_b))

    xp = _layer_norm(x_prompt.reshape(mp, d), ln_in_g[None, :], ln_in_b[None, :], tm)
    xs = _layer_norm(x_sample.reshape(ms, d), ln_in_g[None, :], ln_in_b[None, :], ms)
    memp = mem_prompt.reshape(bsz * nmem, d)
    zeros_state = _block_diag_state(jnp.zeros((bsz, ML_HEADS, HEAD_DIM, HEAD_DIM), F32),
                                    jnp.zeros((bsz, ML_HEADS, HEAD_DIM), F32), jnp.zeros((bsz, ML_HEADS), F32))
    k_pool = cache_fa_k.reshape(depth, n_phys, page_size, FA_W)
    v_pool = cache_fa_v.reshape(depth, n_phys, page_size, FA_W)
    lf_pool = jnp.swapaxes(cache_fa_logf, 2, 3)

    pouts = {n: [] for n in ("fa_k", "fa_v", "fa_lf", "mem_k", "mem_v", "sc", "mC", "mn", "mm", "ffn")}
    souts = {n: [] for n in ("fa_k", "fa_v", "fa_lf", "sc", "mC", "mn", "mm", "ffn")}
    for l in range(depth):
        q, k, v, kb, vb, cv, ml, gates = _inproj(xp, w_in_p[l], b_in_p[l], tm)
        grow = gates[:, :16].T
        lf_rows = jnp.swapaxes(grow[:FA_HEADS].reshape(FA_HEADS, bsz, seq), 0, 1)
        frow = _cumsum_rows(lf_rows).reshape(bsz, FA_HEADS // 2, 2, seq)
        a_out = _fox_prompt(q, kb, vb, frow, bsz, seq, tq)
        h, cbd, nm, mrow = _mlstm(ml, gates, grow, *zeros_state, bsz, seq // ml_chunk, ml_chunk, True)
        x1, sc_tail = _mixproj(a_out, cv, h, ml, xp, w_o_b[l], sc_conv_w[l], ml_norm_g[l], ln1_g[l], ln1_b[l],
                               tm, tps, None)
        mem_k = _dense(memp, w_xk_b[l], F32, nmem)
        mem_v = _dense(memp, w_xv_b[l], F32, nmem)
        xq = _dense(x1, w_xq_b[l], BF16, tm)
        xo = _xattn(xq, mem_k.reshape(bsz, nmem, d), mem_v.reshape(bsz, nmem, d), tm, tps)
        x2 = _dense_ln(xo, w_xo_b[l], x1, ln2_g[l], ln2_b[l], tm)
        xp, ffn_tail = _ffn(x2, w_gate_b[l], w_up_b[l], w_down_b[l], ffn_conv_w[l], ln3_g[l], ln3_b[l], tm, tps, None)
        mc, mn, mmm = _unpack_state(cbd, nm, mrow)
        for n_, v_ in zip(("fa_k", "fa_v", "fa_lf", "mem_k", "mem_v", "sc", "mC", "mn", "mm", "ffn"),
                          (k.reshape(bsz, seq, FA_HEADS, HEAD_DIM), v.reshape(bsz, seq, FA_HEADS, HEAD_DIM),
                           gates[:, :FA_HEADS].reshape(bsz, seq, FA_HEADS),
                           mem_k.reshape(bsz, nmem, X_HEADS, d // X_HEADS), mem_v.reshape(bsz, nmem, X_HEADS, d // X_HEADS),
                           sc_tail.reshape(bsz, 8, SC_W)[:, 6:], mc, mn, mmm,
                           ffn_tail.reshape(bsz, 8, dff)[:, 6:])):
            pouts[n_].append(v_)
        q, k, v, _, _, cv, ml, gates = _inproj(xs, w_in_p[l], b_in_p[l], ms)
        grow = gates[:, :16].T
        pad_keys = lambda a: jnp.pad(a.reshape(db, dseq, FA_W), ((0, 0), (0, page_size - dseq), (0, 0)))
        lf_new = jnp.swapaxes(grow[:FA_HEADS].reshape(FA_HEADS, db, dseq), 0, 1)
        lf_new = jnp.pad(lf_new, ((0, 0), (0, 0), (0, page_size - dseq)))
        a_out = _fox_sample(page_table, q.astype(F32).reshape(db, dseq, FA_W), pad_keys(k), pad_keys(v), lf_new,
                            k_pool[l], v_pool[l], lf_pool[l], pg).reshape(ms, FA_W)
        padr = lambda a: jnp.pad(a.reshape(db, dseq, -1), ((0, 0), (0, ml_chunk - dseq), (0, 0))).reshape(db * ml_chunk, -1)
        gpad = jnp.pad(gates.reshape(db, dseq, 128), ((0, 0), (0, ml_chunk - dseq), (0, 0)))
        pad_slot = (jnp.arange(ml_chunk) >= dseq)[None, :, None]
        lane = jnp.arange(128)[None, None, :]
        gpad = jnp.where(pad_slot & (lane >= G_MI) & (lane < G_MF), NEG_BIG, gpad).reshape(db * ml_chunk, 128)
        st = _block_diag_state(state_ml_C[l], state_ml_n[l], state_ml_m[l])
        h, cbd, nm, mrow = _mlstm((padr(ml[:, 0:ML_W]), padr(ml[:, ML_W:2 * ML_W]), padr(ml[:, 2 * ML_W:3 * ML_W])),
                                  gpad, gpad[:, :16].T, *st, db, 1, ml_chunk, False)
        h = h.reshape(db, ml_chunk, ML_W)[:, :dseq].reshape(ms, ML_W)
        x1, sc_u = _mixproj(a_out, cv, h, ml, xs, w_o_b[l], sc_conv_w[l], ml_norm_g[l], ln1_g[l], ln1_b[l],
                            ms, 1, _tail_halo(state_sc_conv[l]))
        xq = _dense(x1, w_xq_b[l], F32, ms)
        xo = _xattn(xq, cache_mem_k[l].reshape(db, nmem, d), cache_mem_v[l].reshape(db, nmem, d), dseq, 1)
        x2 = _dense_ln(xo, w_xo_b[l], x1, ln2_g[l], ln2_b[l], ms)
        xs, ffn_g = _ffn(x2, w_gate_b[l], w_up_b[l], w_down_b[l], ffn_conv_w[l], ln3_g[l], ln3_b[l], ms, 1,
                         _tail_halo(state_ffn_conv[l]))
        mc, mn, mmm = _unpack_state(cbd, nm, mrow)
        for n_, v_ in zip(("fa_k", "fa_v", "fa_lf", "sc", "mC", "mn", "mm", "ffn"),
                          (k.reshape(db, dseq, FA_HEADS, HEAD_DIM), v.reshape(db, dseq, FA_HEADS, HEAD_DIM),
                           gates[:, :FA_HEADS].reshape(db, dseq, FA_HEADS),
                           sc_u.reshape(db, dseq, SC_W)[:, dseq - 2:], mc, mn, mmm,
                           ffn_g.reshape(db, dseq, dff)[:, dseq - 2:])):
            souts[n_].append(v_)

    st = lambda lst: jnp.stack(lst)
    return (xp.reshape(bsz, seq, d), xs.reshape(db, dseq, d),
            st(pouts["fa_k"]), st(pouts["fa_v"]), st(pouts["fa_lf"]), st(pouts["mem_k"]), st(pouts["mem_v"]),
            st(pouts["sc"]), st(pouts["mC"]), st(pouts["mn"]), st(pouts["mm"]), st(pouts["ffn"]),
            st(souts["fa_k"]), st(souts["fa_v"]), st(souts["fa_lf"]), st(souts["sc"]), st(souts["mC"]),
            st(souts["mn"]), st(souts["mm"]), st(souts["ffn"]))
```
